```python
import math
import jax
import jax.numpy as jnp
from jax import lax
import numpy as np

D_MODEL = 1024
BATCH = 32
SEQ = 2048
DEPTH = 2

N_A_LAYERS = DEPTH // 2
N_B_LAYERS = DEPTH - N_A_LAYERS
ALPHA = (2.0 * DEPTH) ** 0.25
BETA = (8.0 * DEPTH) ** -0.25
LN_EPS = 1e-5
RMS_EPS = 1e-6
ROPE_THETA = 10000.0
NEG_INF = -1e30
FORCE_SCORE = 1e9
TINY = 1e-30

A_HEADS = 16
A_NOPE = 64
A_ROPE = 32
A_V = 64
A_Q_LORA = D_MODEL // 2
A_KV_LORA = D_MODEL // 4
A_Q_BLOCK = 128

B_HEADS = 16
B_GROUPS = 4
B_HPG = B_HEADS // B_GROUPS
B_DK = 64
B_DV = 64
CMP_LEN = 32
CMP_STRIDE = 16
CMP_HID = 4 * B_DK
SEL_LEN = 64
SEL_N = 16
SEL_LOCAL = 2
WINDOW = 512
B_Q_BLOCK = 32

PEER_HEADS = 8
PEER_TOPK = 16
N_KEYS = 128
N_EXPERTS = N_KEYS * N_KEYS
PEER_DK = 256
PEER_CHUNK = 128

kernel_name = "yoco_mla_nsa_peer_deepnorm"


def layer_norm(x, g, b):
    xf = x.astype(jnp.float32)
    mu = jnp.mean(xf, axis=-1, keepdims=True)
    var = jnp.mean(jnp.square(xf - mu), axis=-1, keepdims=True)
    return ((xf - mu) * lax.rsqrt(var + LN_EPS) * g + b).astype(x.dtype)


def rms_norm(x, g):
    xf = x.astype(jnp.float32)
    r = lax.rsqrt(jnp.mean(jnp.square(xf), axis=-1, keepdims=True) + RMS_EPS)
    return (xf * r * g).astype(x.dtype)


def rope(x):
    s, d = x.shape[1], x.shape[-1]
    inv = ROPE_THETA ** (-jnp.arange(0, d, 2, dtype=jnp.float32) / d)
    ang = jnp.arange(s, dtype=jnp.float32)[:, None] * inv[None, :]
    cos = jnp.cos(ang)[None, :, None, :]
    sin = jnp.sin(ang)[None, :, None, :]
    xf = x.astype(jnp.float32)
    x1, x2 = xf[..., : d // 2], xf[..., d // 2:]
    return jnp.concatenate([x1 * cos - x2 * sin, x2 * cos + x1 * sin], axis=-1).astype(x.dtype)


def masked_softmax(sc, mask):
    sc = jnp.where(mask, sc, NEG_INF)
    m = jnp.max(sc, axis=-1, keepdims=True)
    e = jnp.exp(sc - m) * mask
    return e / jnp.maximum(jnp.sum(e, axis=-1, keepdims=True), TINY)


def to_blocks(t, block):
    b, s = t.shape[:2]
    return jnp.swapaxes(t.reshape((b, s // block, block) + t.shape[2:]), 0, 1)


def from_blocks(t):
    t = jnp.swapaxes(t, 0, 1)
    return t.reshape((t.shape[0], t.shape[1] * t.shape[2]) + t.shape[3:])


def mla_mixer(x, w_in, q_norm, kv_norm, w_q_up, w_kv_up, w_o):
    b, s, _ = x.shape
    proj = x @ w_in
    c_q = rms_norm(proj[..., :A_Q_LORA], q_norm)
    c_kv = rms_norm(proj[..., A_Q_LORA:A_Q_LORA + A_KV_LORA], kv_norm)
    k_pe = rope(proj[..., A_Q_LORA + A_KV_LORA:][:, :, None, :])[:, :, 0, :]
    q = (c_q @ w_q_up).reshape(b, s, A_HEADS, A_NOPE + A_ROPE)
    q_nope, q_pe = q[..., :A_NOPE], rope(q[..., A_NOPE:])
    kv = (c_kv @ w_kv_up).reshape(b, s, A_HEADS, A_NOPE + A_V)
    k_nope, v = kv[..., :A_NOPE], kv[..., A_NOPE:]
    scale = (A_NOPE + A_ROPE) ** -0.5
    key_pos = jnp.arange(s)

    def attend_block(args):
        i, qn, qp = args
        sc = (jnp.einsum("bqhd,bkhd->bhqk", qn, k_nope, preferred_element_type=jnp.float32)
              + jnp.einsum("bqhr,bkr->bhqk", qp, k_pe, preferred_element_type=jnp.float32)) * scale
        q_pos = i * A_Q_BLOCK + jnp.arange(A_Q_BLOCK)
        sc = jnp.where(key_pos[None, :] <= q_pos[:, None], sc, NEG_INF)
        p = jax.nn.softmax(sc, axis=-1).astype(v.dtype)
        return jnp.einsum("bhqk,bkhd->bqhd", p, v)

    o = lax.map(attend_block, (jnp.arange(s // A_Q_BLOCK), to_blocks(q_nope, A_Q_BLOCK), to_blocks(q_pe, A_Q_BLOCK)))
    return from_blocks(o).reshape(b, s, A_HEADS * A_V) @ w_o


def compress_tokens(tok, pos_emb, w1, b1, w2):
    b, s, g, d = tok.shape
    r = CMP_LEN // CMP_STRIDE
    nb = s // CMP_STRIDE
    n_cmp = nb - r + 1
    tb = tok.reshape(b, nb, CMP_STRIDE, g, d)
    blocks = jnp.concatenate([tb[:, j:j + n_cmp] for j in range(r)], axis=2)
    blocks = blocks + pos_emb[None, None, :, None, :]
    flat = blocks.transpose(0, 1, 3, 2, 4).reshape(b, n_cmp, g, CMP_LEN * d)
    hid = jax.nn.gelu(flat @ w1 + b1, approximate=False)
    return hid @ w2


def nsa_shared_kv(h, w_kv, pos_k, pos_v, ck_w1, ck_b1, ck_w2, cv_w1, cv_b1, cv_w2):
    b, s, _ = h.shape
    kv = (h @ w_kv).reshape(b, s, 3, B_GROUPS, B_DK + B_DV)
    k_cmp = compress_tokens(kv[:, :, 0, :, :B_DK], pos_k, ck_w1, ck_b1, ck_w2)
    v_cmp = compress_tokens(kv[:, :, 0, :, B_DK:], pos_v, cv_w1, cv_b1, cv_w2)
    n_blk = s // SEL_LEN
    k_slc = rope(kv[:, :, 1, :, :B_DK]).reshape(b, n_blk, SEL_LEN, B_GROUPS, B_DK).transpose(0, 3, 1, 2, 4)
    v_slc = kv[:, :, 1, :, B_DK:].reshape(b, n_blk, SEL_LEN, B_GROUPS, B_DV).transpose(0, 3, 1, 2, 4)
    pad = ((0, 0), (WINDOW, 0), (0, 0), (0, 0))
    k_win = jnp.pad(rope(kv[:, :, 2, :, :B_DK]), pad)
    v_win = jnp.pad(kv[:, :, 2, :, B_DK:], pad)
    return (k_cmp, v_cmp, k_slc, v_slc, k_win, v_win)


def nsa_mixer(x, shared, w_in, w_o):
    k_cmp, v_cmp, k_slc, v_slc, k_win, v_win = shared
    b, s, _ = x.shape
    proj = x @ w_in
    q = proj[..., :B_HEADS * B_DK].reshape(b, s, B_HEADS, B_DK)
    q_rot = rope(q).reshape(b, s, B_GROUPS, B_HPG, B_DK)
    q = q.reshape(b, s, B_GROUPS, B_HPG, B_DK)
    gates = jax.nn.sigmoid(proj[..., B_HEADS * B_DK:].astype(jnp.float32)).astype(x.dtype)
    gates = gates.reshape(b, s, 3, B_GROUPS, B_HPG)
    scale = B_DK ** -0.5
    n_cmp = k_cmp.shape[1]
    n_blk = k_slc.shape[2]
    n_sel = min(SEL_N, n_blk)
    cmp_start = jnp.arange(n_cmp) * CMP_STRIDE
    cmp_end = cmp_start + CMP_LEN - 1
    blk = jnp.arange(n_blk)
    blk_start = blk * SEL_LEN
    overlap = ((cmp_start[:, None] < blk_start[None, :] + SEL_LEN)
               & (cmp_end[:, None] >= blk_start[None, :])).astype(jnp.float32)
    b_idx = jnp.arange(b)[:, None, None, None]
    g_idx = jnp.arange(B_GROUPS)[None, :, None, None]
    win_off = jnp.arange(WINDOW + B_Q_BLOCK) - WINDOW

    def attend_block(args):
        i, qb, qrb, gb = args
        t = i * B_Q_BLOCK + jnp.arange(B_Q_BLOCK)
        sc = jnp.einsum("bqghd,bcgd->bghqc", qb, k_cmp, preferred_element_type=jnp.float32) * scale
        p_c = masked_softmax(sc, cmp_end[None, :] <= t[:, None])
        o_c = jnp.einsum("bghqc,bcgd->bqghd", p_c.astype(v_cmp.dtype), v_cmp)
        imp = jnp.einsum("bghqc,cn->bgqn", p_c, overlap)
        cur = t // SEL_LEN
        forced = (blk[None, :] == 0) | ((blk[None, :] <= cur[:, None]) & (blk[None, :] > cur[:, None] - SEL_LOCAL))
        imp = jnp.where(forced, FORCE_SCORE, imp)
        imp = jnp.where(blk[None, :] <= cur[:, None], imp, NEG_INF)
        _, sel = lax.top_k(imp, n_sel)
        k_g = k_slc[b_idx, g_idx, sel].reshape(b, B_GROUPS, B_Q_BLOCK, n_sel * SEL_LEN, B_DK)
        v_g = v_slc[b_idx, g_idx, sel].reshape(b, B_GROUPS, B_Q_BLOCK, n_sel * SEL_LEN, B_DV)
        tok = (sel[..., None] * SEL_LEN + jnp.arange(SEL_LEN)).reshape(b, B_GROUPS, 1, B_Q_BLOCK, n_sel * SEL_LEN)
        sc = jnp.einsum("bqghd,bgqkd->bghqk", qrb, k_g, preferred_element_type=jnp.float32) * scale
        p_s = masked_softmax(sc, tok <= t[:, None])
        o_s = jnp.einsum("bghqk,bgqkd->bqghd", p_s.astype(v_g.dtype), v_g)
        k_w = lax.dynamic_slice_in_dim(k_win, i * B_Q_BLOCK, WINDOW + B_Q_BLOCK, axis=1)
        v_w = lax.dynamic_slice_in_dim(v_win, i * B_Q_BLOCK, WINDOW + B_Q_BLOCK, axis=1)
        key_pos = i * B_Q_BLOCK + win_off
        m_w = ((key_pos[None, :] >= 0) & (key_pos[None, :] <= t[:, None])
               & (key_pos[None, :] > t[:, None] - WINDOW))
        sc = jnp.einsum("bqghd,bkgd->bghqk", qrb, k_w, preferred_element_type=jnp.float32) * scale
        p_w = masked_softmax(sc, m_w)
        o_w = jnp.einsum("bghqk,bkgd->bqghd", p_w.astype(v_w.dtype), v_w)
        return (gb[:, :, 0, :, :, None] * o_c + gb[:, :, 1, :, :, None] * o_s
                + gb[:, :, 2, :, :, None] * o_w)

    o = lax.map(attend_block, (jnp.arange(s // B_Q_BLOCK), to_blocks(q, B_Q_BLOCK),
                               to_blocks(q_rot, B_Q_BLOCK), to_blocks(gates, B_Q_BLOCK)))
    return from_blocks(o).reshape(b, s, B_HEADS * B_DV) @ w_o


def peer_ffn(x, w_q, subkeys, u, v):
    b, s, d = x.shape
    x_chunks = x.reshape(b * s // PEER_CHUNK, PEER_CHUNK, d)

    def chunk(xc):
        q = (xc @ w_q).reshape(PEER_CHUNK, PEER_HEADS, 2, PEER_DK // 2)
        sc = jnp.einsum("thpd,hpnd->thpn", q, subkeys, preferred_element_type=jnp.float32)
        s_half, i_half = lax.top_k(sc, PEER_TOPK)
        cand = (s_half[:, :, 0, :, None] + s_half[:, :, 1, None, :]).reshape(PEER_CHUNK, PEER_HEADS, PEER_TOPK * PEER_TOPK)
        cand_idx = (i_half[:, :, 0, :, None] * N_KEYS + i_half[:, :, 1, None, :]).reshape(PEER_CHUNK, PEER_HEADS, PEER_TOPK * PEER_TOPK)
        best, pos = lax.top_k(cand, PEER_TOPK)
        experts = jnp.take_along_axis(cand_idx, pos, axis=-1)
        g = jax.nn.softmax(best, axis=-1)
        u_e = u[experts]
        v_e = v[experts]
        a = jax.nn.gelu(jnp.einsum("thkd,td->thk", u_e, xc, preferred_element_type=jnp.float32), approximate=False)
        return jnp.einsum("thk,thkd->td", (g * a).astype(v.dtype), v_e)

    return lax.map(chunk, x_chunks).reshape(b, s, d)


def setup_inputs(seed: int = 0) -> dict:
    key = jax.random.key(seed)
    ks = jax.random.split(key, 24)

    def nrm(k, shape, scale):
        return jax.random.normal(k, shape, jnp.float32) * scale

    return {
        "x": nrm(ks[0], (BATCH, SEQ, D_MODEL), 1.0),
        "a_w_in": nrm(ks[1], (N_A_LAYERS, D_MODEL, A_Q_LORA + A_KV_LORA + A_ROPE), D_MODEL ** -0.5),
        "a_q_norm": 1.0 + nrm(ks[2], (N_A_LAYERS, A_Q_LORA), 0.02),
        "a_kv_norm": 1.0 + nrm(ks[3], (N_A_LAYERS, A_KV_LORA), 0.02),
        "a_w_q_up": nrm(ks[4], (N_A_LAYERS, A_Q_LORA, A_HEADS * (A_NOPE + A_ROPE)), A_Q_LORA ** -0.5),
        "a_w_kv_up": nrm(ks[5], (N_A_LAYERS, A_KV_LORA, A_HEADS * (A_NOPE + A_V)), A_KV_LORA ** -0.5),
        "a_w_o": nrm(ks[6], (N_A_LAYERS, A_HEADS * A_V, D_MODEL), BETA * (A_HEADS * A_V) ** -0.5),
        "b_w_in": nrm(ks[7], (N_B_LAYERS, D_MODEL, B_HEADS * B_DK + 3 * B_HEADS), D_MODEL ** -0.5),
        "b_w_o": nrm(ks[8], (N_B_LAYERS, B_HEADS * B_DV, D_MODEL), BETA * (B_HEADS * B_DV) ** -0.5),
        "s_w_kv": nrm(ks[9], (D_MODEL, 3 * B_GROUPS * (B_DK + B_DV)), D_MODEL ** -0.5),
        "s_cmp_pos_k": nrm(ks[10], (CMP_LEN, B_DK), 0.02),
        "s_cmp_pos_v": nrm(ks[11], (CMP_LEN, B_DV), 0.02),
        "s_cmp_k_w1": nrm(ks[12], (CMP_LEN * B_DK, CMP_HID), (CMP_LEN * B_DK) ** -0.5),
        "s_cmp_k_b1": nrm(ks[13], (CMP_HID,), 0.02),
        "s_cmp_k_w2": nrm(ks[14], (CMP_HID, B_DK), CMP_HID ** -0.5),
        "s_cmp_v_w1": nrm(ks[15], (CMP_LEN * B_DV, CMP_HID), (CMP_LEN * B_DV) ** -0.5),
        "s_cmp_v_b1": nrm(ks[16], (CMP_HID,), 0.02),
        "s_cmp_v_w2": nrm(ks[17], (CMP_HID, B_DV), CMP_HID ** -0.5),
        "p_w_q": nrm(ks[18], (DEPTH, D_MODEL, PEER_HEADS * PEER_DK), D_MODEL ** -0.5),
        "p_subkeys": nrm(ks[19], (DEPTH, PEER_HEADS, 2, N_KEYS, PEER_DK // 2), (PEER_DK // 2) ** -0.5),
        "p_u": nrm(ks[20], (DEPTH, N_EXPERTS, D_MODEL), D_MODEL ** -0.5),
        "p_v": nrm(ks[21], (DEPTH, N_EXPERTS, D_MODEL), BETA * PEER_HEADS ** -0.5),
        "ln_g": 1.0 + nrm(ks[22], (DEPTH, 2, D_MODEL), 0.02),
        "ln_b": nrm(ks[23], (DEPTH, 2, D_MODEL), 0.02),
    }


def reference(x, a_w_in, a_q_norm, a_kv_norm, a_w_q_up, a_w_kv_up, a_w_o, b_w_in, b_w_o,
              s_w_kv, s_cmp_pos_k, s_cmp_pos_v, s_cmp_k_w1, s_cmp_k_b1, s_cmp_k_w2,
              s_cmp_v_w1, s_cmp_v_b1, s_cmp_v_w2, p_w_q, p_subkeys, p_u, p_v, ln_g, ln_b):
    h = x
    shared = None
    for layer in range(DEPTH):
        if layer < N_A_LAYERS:
            mix = mla_mixer(h, a_w_in[layer], a_q_norm[layer], a_kv_norm[layer],
                            a_w_q_up[layer], a_w_kv_up[layer], a_w_o[layer])
        else:
            if layer == N_A_LAYERS:
                shared = nsa_shared_kv(h, s_w_kv, s_cmp_pos_k, s_cmp_pos_v, s_cmp_k_w1, s_cmp_k_b1,
                                       s_cmp_k_w2, s_cmp_v_w1, s_cmp_v_b1, s_cmp_v_w2)
            j = layer - N_A_LAYERS
            mix = nsa_mixer(h, shared, b_w_in[j], b_w_o[j])
        h = layer_norm(ALPHA * h + mix, ln_g[layer, 0], ln_b[layer, 0])
        ffn = peer_ffn(h, p_w_q[layer], p_subkeys[layer], p_u[layer], p_v[layer])
        h = layer_norm(ALPHA * h + ffn, ln_g[layer, 1], ln_b[layer, 1])
    return h
```

```python
import functools
import math

import jax
import jax.numpy as jnp
from jax import lax
from jax.experimental import pallas as pl
from jax.experimental.pallas import tpu as pltpu

F32 = jnp.float32
BF16 = jnp.bfloat16

LANES = 128
SUBLANES = 8
V7X_VMEM_LIMIT_BYTES = 56 * 1024 * 1024

D_MODEL = 1024
DEPTH = 2
ALPHA = (2.0 * DEPTH) ** 0.25
LN_EPS = 1e-5
RMS_EPS = 1e-6
ROPE_THETA = 10000.0
NEG_INF = -1e30
FORCE_SCORE = 1e9
TINY = 1e-30

A_HEADS = 16
A_NOPE = 64
A_ROPE = 32
A_V = 64
A_Q_LORA = 512
A_KV_LORA = 256

B_HEADS = 16
B_GROUPS = 4
B_HPG = 4
B_DK = 64
B_DV = 64
CMP_LEN = 32
CMP_STRIDE = 16
CMP_HID = 256
SEL_LEN = 64
SEL_SHIFT = 6
SEL_N = 16
SEL_LOCAL = 2
WINDOW = 512

PEER_HEADS = 8
PEER_TOPK = 16
N_KEYS = 128
PEER_HALF = 128

_NT = (((1,), (1,)), ((), ()))


def _params(sem):
    return pltpu.CompilerParams(dimension_semantics=sem, vmem_limit_bytes=V7X_VMEM_LIMIT_BYTES)


def _layer_norm(y, g, b):
    mu = jnp.mean(y, axis=-1, keepdims=True)
    yc = y - mu
    var = jnp.mean(yc * yc, axis=-1, keepdims=True)
    return yc * lax.rsqrt(var + LN_EPS) * g + b


def _rms_norm(y, g):
    r = lax.rsqrt(jnp.mean(y * y, axis=-1, keepdims=True) + RMS_EPS)
    return y * r * g


def _rope_block(xb, c, s_lo, s_hi, half):
    up = pltpu.roll(xb, LANES - half, 1)
    dn = pltpu.roll(xb, half, 1)
    return xb * c + up * s_lo + dn * s_hi


def _rope_tables(seq, d_rot, lane_offsets, block_pass_lanes):
    half = d_rot // 2
    inv = ROPE_THETA ** (-jnp.arange(0, d_rot, 2, dtype=F32) / d_rot)
    ang = jnp.arange(seq, dtype=F32)[:, None] * inv[None, :]
    cos, sin = jnp.cos(ang), jnp.sin(ang)
    c = jnp.full((seq, LANES), 1.0 if block_pass_lanes else 0.0, F32)
    s_lo = jnp.zeros((seq, LANES), F32)
    s_hi = jnp.zeros((seq, LANES), F32)
    for off in lane_offsets:
        c = c.at[:, off:off + half].set(cos).at[:, off + half:off + d_rot].set(cos)
        s_lo = s_lo.at[:, off:off + half].set(-sin)
        s_hi = s_hi.at[:, off + half:off + d_rot].set(sin)
    return c, s_lo, s_hi


def _mla_proj_kernel(x_ref, win_ref, qn_ref, kvn_ref, wq_ref, wk_ref, wv_ref,
                     c_ref, slo_ref, shi_ref, q_out, k_out, v_out):
    x = x_ref[...].astype(BF16)
    proj = jnp.dot(x, win_ref[...], preferred_element_type=F32)
    c_q = _rms_norm(proj[:, :A_Q_LORA], qn_ref[...]).astype(BF16)
    c_kv = _rms_norm(proj[:, A_Q_LORA:A_Q_LORA + A_KV_LORA], kvn_ref[...]).astype(BF16)
    c, slo, shi = c_ref[...], slo_ref[...], shi_ref[...]
    kpe = _rope_block(proj[:, A_Q_LORA + A_KV_LORA:], c, slo, shi, A_ROPE // 2)
    q = jnp.dot(c_q, wq_ref[...], preferred_element_type=F32)
    kn = jnp.dot(c_kv, wk_ref[...], preferred_element_type=F32)
    scale = (A_NOPE + A_ROPE) ** -0.5
    for h in range(A_HEADS):
        sl = slice(h * LANES, (h + 1) * LANES)
        q_out[:, sl] = (_rope_block(q[:, sl], c, slo, shi, A_ROPE // 2) * scale).astype(BF16)
        k_out[:, sl] = (kn[:, sl] + kpe).astype(BF16)
    v_out[...] = jnp.dot(c_kv, wv_ref[...], preferred_element_type=F32).astype(BF16)


def _mla_proj(x2, w_in, q_norm, kv_norm, w_q_up, w_kv_up, seq, tm=512):
    t = x2.shape[0]
    win_p = jnp.zeros((D_MODEL, A_Q_LORA + A_KV_LORA + LANES), F32)
    win_p = win_p.at[:, :A_Q_LORA + A_KV_LORA].set(w_in[:, :A_Q_LORA + A_KV_LORA])
    win_p = win_p.at[:, A_Q_LORA + A_KV_LORA + A_NOPE:A_Q_LORA + A_KV_LORA + A_NOPE + A_ROPE].set(
        w_in[:, A_Q_LORA + A_KV_LORA:])
    wq = jnp.pad(w_q_up.reshape(A_Q_LORA, A_HEADS, A_NOPE + A_ROPE),
                 ((0, 0), (0, 0), (0, LANES - A_NOPE - A_ROPE))).reshape(A_Q_LORA, A_HEADS * LANES)
    wkv = w_kv_up.reshape(A_KV_LORA, A_HEADS, A_NOPE + A_V)
    wk = jnp.pad(wkv[:, :, :A_NOPE], ((0, 0), (0, 0), (0, LANES - A_NOPE))).reshape(A_KV_LORA, A_HEADS * LANES)
    wv = wkv[:, :, A_NOPE:].reshape(A_KV_LORA, A_HEADS * A_V)
    c, slo, shi = _rope_tables(seq, A_ROPE, (A_NOPE,), True)
    nsb = seq // tm
    full = lambda shp: pl.BlockSpec(shp, lambda i: (0, 0))
    tab = pl.BlockSpec((tm, LANES), lambda i: (i % nsb, 0))
    return pl.pallas_call(
        _mla_proj_kernel,
        grid=(t // tm,),
        in_specs=[pl.BlockSpec((tm, D_MODEL), lambda i: (i, 0)),
                  full(win_p.shape), full((1, A_Q_LORA)), full((1, A_KV_LORA)),
                  full(wq.shape), full(wk.shape), full(wv.shape), tab, tab, tab],
        out_specs=[pl.BlockSpec((tm, A_HEADS * LANES), lambda i: (i, 0)),
                   pl.BlockSpec((tm, A_HEADS * LANES), lambda i: (i, 0)),
                   pl.BlockSpec((tm, A_HEADS * A_V), lambda i: (i, 0))],
        out_shape=[jax.ShapeDtypeStruct((t, A_HEADS * LANES), BF16),
                   jax.ShapeDtypeStruct((t, A_HEADS * LANES), BF16),
                   jax.ShapeDtypeStruct((t, A_HEADS * A_V), BF16)],
        compiler_params=_params(("parallel",)),
        name="mla_proj",
    )(x2, win_p.astype(BF16), q_norm.reshape(1, -1), kv_norm.reshape(1, -1),
      wq.astype(BF16), wk.astype(BF16), wv.astype(BF16), c, slo, shi)


def _flash_update(s, mask, v, m_ref, l_ref, acc_ref):
    s = jnp.where(mask, s, NEG_INF)
    m_old = m_ref[...]
    m_new = jnp.maximum(m_old, jnp.max(s, axis=1, keepdims=True))
    p = jnp.where(mask, jnp.exp(s - m_new), 0.0)
    alpha = jnp.exp(m_old - m_new)
    l_ref[...] = alpha * l_ref[...] + jnp.sum(p, axis=1, keepdims=True)
    acc_ref[...] = alpha * acc_ref[...] + jnp.dot(p.astype(BF16), v, preferred_element_type=F32)
    m_ref[...] = m_new


def _mla_attn_kernel(q_ref, k_ref, v_ref, o_ref, m_ref, l_ref, acc_ref, *, tq, tk):
    qi = pl.program_id(2)
    row = qi * tq + lax.broadcasted_iota(jnp.int32, (tq, 1), 0)
    n_kv = (qi * tq + tq + tk - 1) // tk
    outs = []
    for hh in range(2):
        q = q_ref[:, hh * LANES:(hh + 1) * LANES]
        m_ref[...] = jnp.full(m_ref.shape, NEG_INF, F32)
        l_ref[...] = jnp.zeros(l_ref.shape, F32)
        acc_ref[...] = jnp.zeros(acc_ref.shape, F32)

        def body(j, carry, q=q, hh=hh):
            k = k_ref[pl.ds(pl.multiple_of(j * tk, tk), tk), hh * LANES:(hh + 1) * LANES]
            v = v_ref[pl.ds(pl.multiple_of(j * tk, tk), tk), :]
            s = lax.dot_general(q, k, _NT, preferred_element_type=F32)
            col = j * tk + lax.broadcasted_iota(jnp.int32, (1, tk), 1)
            _flash_update(s, col <= row, v, m_ref, l_ref, acc_ref)
            return carry

        lax.fori_loop(0, n_kv, body, 0)
        outs.append(acc_ref[...] / l_ref[...])
    lane = lax.broadcasted_iota(jnp.int32, (1, LANES), 1)
    o_ref[...] = jnp.where(lane < A_V, outs[0], outs[1]).astype(o_ref.dtype)


def _mla_attn(qp, kp, vp, batch, seq, tq=256, tk=512):
    t = qp.shape[0]
    nq = seq // tq
    npair = A_HEADS // 2
    return pl.pallas_call(
        functools.partial(_mla_attn_kernel, tq=tq, tk=tk),
        grid=(batch, npair, nq),
        in_specs=[pl.BlockSpec((tq, 2 * LANES), lambda b, p, i: (b * nq + i, p)),
                  pl.BlockSpec((seq, 2 * LANES), lambda b, p, i: (b, p)),
                  pl.BlockSpec((seq, LANES), lambda b, p, i: (b, p))],
        out_specs=pl.BlockSpec((tq, LANES), lambda b, p, i: (b * nq + i, p)),
        out_shape=jax.ShapeDtypeStruct((t, A_HEADS * A_V), BF16),
        scratch_shapes=[pltpu.VMEM((tq, 1), F32), pltpu.VMEM((tq, 1), F32), pltpu.VMEM((tq, LANES), F32)],
        compiler_params=_params(("parallel", "parallel", "arbitrary")),
        name="mla_attn",
    )(qp, kp, vp)


def _proj_ln_kernel(o_ref, w_ref, res_ref, g_ref, b_ref, out_ref):
    mix = jnp.dot(o_ref[...], w_ref[...], preferred_element_type=F32)
    out_ref[...] = _layer_norm(ALPHA * res_ref[...] + mix, g_ref[...], b_ref[...])


def _proj_ln(o, w, res, g, b, tm=512):
    t = o.shape[0]
    row = lambda i: (i, 0)
    full = lambda shp: pl.BlockSpec(shp, lambda i: (0, 0))
    return pl.pallas_call(
        _proj_ln_kernel,
        grid=(t // tm,),
        in_specs=[pl.BlockSpec((tm, D_MODEL), row), full(w.shape), pl.BlockSpec((tm, D_MODEL), row),
                  full((1, D_MODEL)), full((1, D_MODEL))],
        out_specs=pl.BlockSpec((tm, D_MODEL), row),
        out_shape=jax.ShapeDtypeStruct((t, D_MODEL), F32),
        compiler_params=_params(("parallel",)),
        name="proj_ln",
    )(o, w.astype(BF16), res, g.reshape(1, -1), b.reshape(1, -1))


_PEER_NEXT = PEER_TOPK + 1
_PEER_CAND = [(a, b) for a in range(_PEER_NEXT) for b in range(_PEER_NEXT) if (a + 1) * (b + 1) <= _PEER_NEXT]
_PEER_CAND_ROWS = ((len(_PEER_CAND) + 7) // 8) * 8
_BIG_NEG = -3.0e38


def _extract_max_rows(cur, n, dst_ref):
    for k in range(n):
        m = jnp.max(cur, axis=0, keepdims=True)
        dst_ref[k:k + 1, :] = m
        cur = jnp.where(cur == m, _BIG_NEG, cur)


def _peer_route(h_ref, wqT_ref, sk_ref, hT_sc, qT_sc, s2_sc, e2_sc, thr_sc, e1_sc, top_sc, cand_sc, best_sc):
    hT = h_ref[...].T.astype(BF16)
    hT_sc[...] = hT
    qT_sc[...] = jnp.dot(wqT_ref[...], hT, preferred_element_type=F32).astype(BF16)
    cand_sc[...] = jnp.full(cand_sc.shape, _BIG_NEG, F32)

    def head(hh, carry):
        s = []
        for p in range(2):
            qb = qT_sc[pl.ds(pl.multiple_of((hh * 2 + p) * PEER_HALF, PEER_HALF), PEER_HALF), :]
            sp = jnp.dot(sk_ref[hh * 2 + p], qb, preferred_element_type=F32)
            _extract_max_rows(sp, _PEER_NEXT, top_sc.at[p])
            s.append(sp)
        for r, (a, b) in enumerate(_PEER_CAND):
            cand_sc[r:r + 1, :] = top_sc[0, a:a + 1, :] + top_sc[1, b:b + 1, :]
        _extract_max_rows(cand_sc[...], _PEER_NEXT, best_sc)
        m = best_sc[0:1, :]
        z = jnp.sum(jnp.exp(best_sc[0:PEER_TOPK, :] - m), axis=0, keepdims=True)
        tau = 0.5 * (best_sc[PEER_TOPK - 1:PEER_TOPK, :] + best_sc[PEER_TOPK:PEER_TOPK + 1, :])
        thr_sc[hh] = tau - s[0]
        e1_sc[hh] = jnp.exp(s[0] - top_sc[0, 0:1, :])
        e2_sc[hh] = jnp.exp(s[1] - top_sc[1, 0:1, :]) / z
        s2_sc[hh] = s[1]
        return carry

    lax.fori_loop(0, PEER_HEADS, head, 0)


def _peer_kernel(h_ref, wqT_ref, sk_ref, u_ref, vT_ref, g_ref, b_ref, out_ref,
                 hT_sc, qT_sc, s2_sc, e2_sc, thr_sc, e1_sc, top_sc, cand_sc, best_sc, gate_sc, acc_sc,
                 *, tt, eb):
    e = pl.program_id(1)

    @pl.when(e == 0)
    def _():
        _peer_route(h_ref, wqT_ref, sk_ref, hT_sc, qT_sc, s2_sc, e2_sc, thr_sc, e1_sc, top_sc, cand_sc, best_sc)
        acc_sc[...] = jnp.zeros(acc_sc.shape, F32)

    hT = hT_sc[...]
    nblk = eb // N_KEYS
    rows = pl.ds(pl.multiple_of(e * nblk, nblk), nblk)
    for blk in range(nblk):
        hid = jnp.dot(u_ref[blk * N_KEYS:(blk + 1) * N_KEYS, :], hT, preferred_element_type=F32)
        for lc in range(tt // LANES):
            ls = slice(lc * LANES, (lc + 1) * LANES)
            w = jnp.zeros((N_KEYS, LANES), F32)
            for hh in range(PEER_HEADS):
                thr = thr_sc[hh, rows, ls][blk:blk + 1, :]
                e1 = e1_sc[hh, rows, ls][blk:blk + 1, :]
                w = w + jnp.where(s2_sc[hh, :, ls] >= thr, e2_sc[hh, :, ls] * e1, 0.0)
            hb = hid[:, ls]
            act = 0.5 * hb * (1.0 + lax.erf(hb * (2.0 ** -0.5)))
            gate_sc[blk * N_KEYS:(blk + 1) * N_KEYS, ls] = (act * w).astype(BF16)
    acc_sc[...] += jnp.dot(vT_ref[...], gate_sc[...], preferred_element_type=F32)

    @pl.when(e == pl.num_programs(1) - 1)
    def _():
        y = ALPHA * h_ref[...] + acc_sc[...].T
        out_ref[...] = _layer_norm(y, g_ref[...], b_ref[...])


def _peer(h, w_q, subkeys, u, v, g, b, tt=512, eb=SUBLANES * N_KEYS):
    t = h.shape[0]
    n_exp = u.shape[0]
    assert eb == SUBLANES * N_KEYS and n_exp == N_KEYS * N_KEYS
    wqT = w_q.T.astype(BF16)
    sk = subkeys.reshape(PEER_HEADS * 2, N_KEYS, PEER_HALF).astype(BF16)
    vT = v.T.astype(BF16)
    return pl.pallas_call(
        functools.partial(_peer_kernel, tt=tt, eb=eb),
        grid=(t // tt, n_exp // eb),
        in_specs=[pl.BlockSpec((tt, D_MODEL), lambda i, e: (i, 0)),
                  pl.BlockSpec(wqT.shape, lambda i, e: (0, 0)),
                  pl.BlockSpec(sk.shape, lambda i, e: (0, 0, 0)),
                  pl.BlockSpec((eb, D_MODEL), lambda i, e: (e, 0)),
                  pl.BlockSpec((D_MODEL, eb), lambda i, e: (0, e)),
                  pl.BlockSpec((1, D_MODEL), lambda i, e: (0, 0)),
                  pl.BlockSpec((1, D_MODEL), lambda i, e: (0, 0))],
        out_specs=pl.BlockSpec((tt, D_MODEL), lambda i, e: (i, 0)),
        out_shape=jax.ShapeDtypeStruct((t, D_MODEL), F32),
        scratch_shapes=[pltpu.VMEM((D_MODEL, tt), BF16),
                        pltpu.VMEM((PEER_HEADS * 2 * PEER_HALF, tt), BF16),
                        pltpu.VMEM((PEER_HEADS, N_KEYS, tt), F32),
                        pltpu.VMEM((PEER_HEADS, N_KEYS, tt), F32),
                        pltpu.VMEM((PEER_HEADS, N_KEYS, tt), F32),
                        pltpu.VMEM((PEER_HEADS, N_KEYS, tt), F32),
                        pltpu.VMEM((2, 24, tt), F32),
                        pltpu.VMEM((_PEER_CAND_ROWS, tt), F32),
                        pltpu.VMEM((24, tt), F32),
                        pltpu.VMEM((eb, tt), BF16),
                        pltpu.VMEM((D_MODEL, tt), F32)],
        compiler_params=_params(("parallel", "arbitrary")),
        name="peer",
    )(h, wqT, sk, u.astype(BF16), vT, g.reshape(1, -1), b.reshape(1, -1))


_KV_COLS = 3 * B_GROUPS * (B_DK + B_DV)
_Q_COLS = B_HEADS * B_DK
_GATE_COLS = B_GROUPS * LANES


def _nsa_proj_kernel(h_ref, w_ref, ck_ref, sklo_ref, skhi_ref, cq_ref, sqlo_ref, sqhi_ref,
                     kvc_out, ks_out, vs_out, kw_out, vw_out, q_out, qr_out, gate_out):
    h = h_ref[...].astype(BF16)
    proj = jnp.dot(h, w_ref[...], preferred_element_type=F32)
    nb = B_GROUPS * LANES
    kvc_out[...] = proj[:, :nb]
    ck, sklo, skhi = ck_ref[...], sklo_ref[...], skhi_ref[...]
    lane = lax.broadcasted_iota(jnp.int32, (1, LANES), 1)
    for br, (k_out, v_out) in ((1, (ks_out, vs_out)), (2, (kw_out, vw_out))):
        for g in range(B_GROUPS):
            sl = slice(br * nb + g * LANES, br * nb + (g + 1) * LANES)
            kv = _rope_block(proj[:, sl], ck, sklo, skhi, B_DK // 2)
            sw = pltpu.roll(kv, B_DK, 1)
            gs = slice(g * LANES, (g + 1) * LANES)
            k_out[:, gs] = jnp.where(lane < B_DK, kv, sw).astype(BF16)
            v_out[:, gs] = jnp.where(lane < B_DK, sw, kv).astype(BF16)
    cq, sqlo, sqhi = cq_ref[...], sqlo_ref[...], sqhi_ref[...]
    scale = B_DK ** -0.5
    for j in range(_Q_COLS // LANES):
        sl = slice(_KV_COLS + j * LANES, _KV_COLS + (j + 1) * LANES)
        js = slice(j * LANES, (j + 1) * LANES)
        qb = proj[:, sl]
        q_out[:, js] = (qb * scale).astype(BF16)
        qr_out[:, js] = (_rope_block(qb, cq, sqlo, sqhi, B_DK // 2) * scale).astype(BF16)
    gz = proj[:, _KV_COLS + _Q_COLS:]
    gate_out[...] = 1.0 / (1.0 + jnp.exp(-gz))


def _nsa_proj(h, w_kv, w_in, seq, tm=512):
    t = h.shape[0]
    wg = w_in[:, _Q_COLS:].reshape(D_MODEL, 3, B_GROUPS, B_HPG).transpose(0, 2, 1, 3)
    wg = jnp.pad(wg.reshape(D_MODEL, B_GROUPS, 3 * B_HPG), ((0, 0), (0, 0), (0, LANES - 3 * B_HPG)))
    w = jnp.concatenate([w_kv, w_in[:, :_Q_COLS], wg.reshape(D_MODEL, _GATE_COLS)], axis=1).astype(BF16)
    ck, sklo, skhi = _rope_tables(seq, B_DK, (0,), True)
    cq, sqlo, sqhi = _rope_tables(seq, B_DK, (0, B_DK), True)
    nsb = seq // tm
    row = lambda i: (i, 0)
    tab = pl.BlockSpec((tm, LANES), lambda i: (i % nsb, 0))
    nb = B_GROUPS * LANES
    shapes = [((t, nb), F32)] + [((t, nb), BF16)] * 4 + [((t, _Q_COLS), BF16)] * 2 + [((t, _GATE_COLS), F32)]
    return pl.pallas_call(
        _nsa_proj_kernel,
        grid=(t // tm,),
        in_specs=[pl.BlockSpec((tm, D_MODEL), row), pl.BlockSpec(w.shape, lambda i: (0, 0))] + [tab] * 6,
        out_specs=[pl.BlockSpec((tm, s[1]), row) for s, _ in shapes],
        out_shape=[jax.ShapeDtypeStruct(s, d) for s, d in shapes],
        compiler_params=_params(("parallel",)),
        name="nsa_proj",
    )(h, w, ck, sklo, skhi, cq, sqlo, sqhi)


def _nsa_compress_kernel(x_ref, pos_ref, w1_ref, b1_ref, w2_ref, out_ref):
    x = x_ref[0, 0]
    top = jnp.dot((x + pos_ref[0, 0:1, :]).astype(BF16), w1_ref[0, 0], preferred_element_type=F32)
    bot = jnp.dot((x + pos_ref[0, 1:2, :]).astype(BF16), w1_ref[0, 1], preferred_element_type=F32)
    nrow = x.shape[0]
    hid = top + pltpu.roll(bot, nrow - 1, 0) + b1_ref[0]
    act = 0.5 * hid * (1.0 + lax.erf(hid * (2.0 ** -0.5)))
    out_ref[0, 0] = jnp.dot(act.astype(BF16), w2_ref[0], preferred_element_type=F32).astype(BF16)


def _nsa_compress(kvc, pos_k, pos_v, k_w1, k_b1, k_w2, v_w1, v_b1, v_w2, batch, seq):
    nsb = seq // CMP_STRIDE
    feat = CMP_STRIDE * B_DK
    x = kvc.reshape(batch, nsb, CMP_STRIDE, B_GROUPS, 2, B_DK).transpose(0, 3, 4, 1, 2, 5)
    x = x.reshape(batch, B_GROUPS * 2, nsb, feat)
    pos = jnp.stack([pos_k.reshape(2, feat), pos_v.reshape(2, feat)])
    w1 = jnp.stack([k_w1.reshape(2, feat, CMP_HID), v_w1.reshape(2, feat, CMP_HID)]).astype(BF16)
    b1 = jnp.stack([k_b1.reshape(1, CMP_HID), v_b1.reshape(1, CMP_HID)])
    w2 = jnp.stack([jnp.concatenate([k_w2, k_w2], axis=1), jnp.concatenate([v_w2, v_w2], axis=1)]).astype(BF16)
    return pl.pallas_call(
        _nsa_compress_kernel,
        grid=(batch, B_GROUPS * 2),
        in_specs=[pl.BlockSpec((1, 1, nsb, feat), lambda b, c: (b, c, 0, 0)),
                  pl.BlockSpec((1, 2, feat), lambda b, c: (c % 2, 0, 0)),
                  pl.BlockSpec((1, 2, feat, CMP_HID), lambda b, c: (c % 2, 0, 0, 0)),
                  pl.BlockSpec((1, 1, CMP_HID), lambda b, c: (c % 2, 0, 0)),
                  pl.BlockSpec((1, CMP_HID, LANES), lambda b, c: (c % 2, 0, 0))],
        out_specs=pl.BlockSpec((1, 1, nsb, LANES), lambda b, c: (b, c, 0, 0)),
        out_shape=jax.ShapeDtypeStruct((batch, B_GROUPS * 2, nsb, LANES), BF16),
        compiler_params=_params(("parallel", "arbitrary")),
        name="nsa_compress",
    )(x, pos, w1, b1, w2)


def _split_bf16(x):
    hi = x.astype(BF16)
    mid = (x - hi.astype(F32)).astype(BF16)
    lo = (x - hi.astype(F32) - mid.astype(F32)).astype(BF16)
    return hi, mid, lo


def _nsa_attn_kernel(q_ref, qr_ref, gate_ref, ks_ref, vs_ref, kw_ref, vw_ref, cmp_ref, o_ref,
                     m_ref, l_ref, acc_ref, *, tq, tk, seq):
    qi = pl.program_id(2)
    t0 = qi * tq
    row = t0 + lax.broadcasted_iota(jnp.int32, (tq, 1), 0)
    lane = lax.broadcasted_iota(jnp.int32, (1, LANES), 1)
    n_cmp = seq // CMP_STRIDE - CMP_LEN // CMP_STRIDE + 1
    n_blk = seq // SEL_LEN
    kc = cmp_ref[0, 0]
    vc = cmp_ref[0, 1]
    gates = gate_ref[...]

    def head_q(ref, head):
        pair, hh = divmod(head, 2)
        qp = ref[:, pair * LANES:(pair + 1) * LANES]
        keep = (lane < B_DK) if hh == 0 else (lane >= B_DK)
        return jnp.where(keep, qp, jnp.zeros_like(qp))

    cmask = (lane * CMP_STRIDE + (CMP_LEN - 1) <= row) & (lane < n_cmp)
    o_heads = []
    psum = jnp.zeros((tq, LANES), F32)
    for head in range(B_HPG):
        s = lax.dot_general(head_q(q_ref, head), kc, _NT, preferred_element_type=F32)
        s = jnp.where(cmask, s, NEG_INF)
        ex = jnp.where(cmask, jnp.exp(s - jnp.max(s, axis=1, keepdims=True)), 0.0)
        p = ex / jnp.maximum(jnp.sum(ex, axis=1, keepdims=True), TINY)
        psum = psum + p
        o_c = jnp.dot(p.astype(BF16), vc, preferred_element_type=F32)
        o_heads.append(gates[:, head:head + 1] * o_c)
    crow = lax.broadcasted_iota(jnp.int32, (LANES, LANES), 0) * CMP_STRIDE
    ncol = lax.broadcasted_iota(jnp.int32, (LANES, LANES), 1) * SEL_LEN
    overlap = jnp.where((crow < ncol + SEL_LEN) & (crow + (CMP_LEN - 1) >= ncol), 1.0, 0.0).astype(BF16)
    imp = sum(jnp.dot(part, overlap, preferred_element_type=F32) for part in _split_bf16(psum))
    cur = jnp.right_shift(row, SEL_SHIFT)
    forced = (lane == 0) | ((lane <= cur) & (lane > cur - SEL_LOCAL))
    imp = jnp.where(forced, FORCE_SCORE, imp)
    imp = jnp.where((lane <= cur) & (lane < n_blk), imp, NEG_INF)
    rank = jnp.zeros((tq, LANES), F32)
    for mblk in range(n_blk):
        other = imp[:, mblk:mblk + 1]
        ahead = (other > imp) | ((other == imp) & (lane > mblk))
        rank = rank + jnp.where(ahead, 1.0, 0.0)
    sel = jnp.where((rank < min(SEL_N, n_blk)) & (lane < n_blk), 1.0, 0.0).astype(BF16)

    def run_branch(qsrc, k_ref, v_ref, j_lo, j_hi, mask_fn, gate_base):
        outs = []
        for head in range(B_HPG):
            q = head_q(qsrc, head)
            m_ref[...] = jnp.full(m_ref.shape, NEG_INF, F32)
            l_ref[...] = jnp.zeros(l_ref.shape, F32)
            acc_ref[...] = jnp.zeros(acc_ref.shape, F32)

            def body(j, carry, q=q):
                ds = pl.ds(pl.multiple_of(j * tk, tk), tk)
                s = lax.dot_general(q, k_ref[ds, :], _NT, preferred_element_type=F32)
                col = j * tk + lax.broadcasted_iota(jnp.int32, (1, tk), 1)
                _flash_update(s, mask_fn(j, col), v_ref[ds, :], m_ref, l_ref, acc_ref)
                return carry

            lax.fori_loop(j_lo, j_hi, body, 0)
            outs.append(gates[:, gate_base + head:gate_base + head + 1] * (acc_ref[...] / l_ref[...]))
        return outs

    j_diag = (t0 + tq + tk - 1) // tk

    def sel_mask(j, col):
        blk_of_key = jnp.right_shift(j * tk + lax.broadcasted_iota(jnp.int32, (LANES, tk), 1), SEL_SHIFT)
        expand = jnp.where(blk_of_key == lax.broadcasted_iota(jnp.int32, (LANES, tk), 0), 1.0, 0.0).astype(BF16)
        chosen = jnp.dot(sel, expand, preferred_element_type=F32)
        return (chosen > 0.5) & (col <= row)

    def win_mask(j, col):
        return (col <= row) & (col > row - WINDOW)

    o_s = run_branch(qr_ref, ks_ref, vs_ref, 0, j_diag, sel_mask, B_HPG)
    j_wlo = jnp.maximum(t0 - (WINDOW - 1), 0) // tk
    o_w = run_branch(qr_ref, kw_ref, vw_ref, j_wlo, j_diag, win_mask, 2 * B_HPG)
    for pair in range(B_HPG // 2):
        tot = [o_heads[2 * pair + hh] + o_s[2 * pair + hh] + o_w[2 * pair + hh] for hh in range(2)]
        o_ref[:, pair * LANES:(pair + 1) * LANES] = jnp.where(lane < B_DV, tot[0], tot[1]).astype(o_ref.dtype)


def _nsa_attn(q, qr, gates, ks, vs, kw, vw, cmp, batch, seq, tq=256, tk=256):
    t = q.shape[0]
    nq = seq // tq
    qcols = B_HPG * B_DK
    qspec = pl.BlockSpec((tq, qcols), lambda b, g, i: (b * nq + i, g))
    kvspec = pl.BlockSpec((seq, LANES), lambda b, g, i: (b, g))
    return pl.pallas_call(
        functools.partial(_nsa_attn_kernel, tq=tq, tk=tk, seq=seq),
        grid=(batch, B_GROUPS, nq),
        in_specs=[qspec, qspec, pl.BlockSpec((tq, LANES), lambda b, g, i: (b * nq + i, g)),
                  kvspec, kvspec, kvspec, kvspec,
                  pl.BlockSpec((1, 2, seq // CMP_STRIDE, LANES), lambda b, g, i: (b, g, 0, 0))],
        out_specs=qspec,
        out_shape=jax.ShapeDtypeStruct((t, B_HEADS * B_DV), BF16),
        scratch_shapes=[pltpu.VMEM((tq, 1), F32), pltpu.VMEM((tq, 1), F32), pltpu.VMEM((tq, LANES), F32)],
        compiler_params=_params(("parallel", "parallel", "arbitrary")),
        name="nsa_attn",
    )(q, qr, gates, ks, vs, kw, vw, cmp)


def kernel(x, a_w_in, a_q_norm, a_kv_norm, a_w_q_up, a_w_kv_up, a_w_o, b_w_in, b_w_o, s_w_kv, s_cmp_pos_k, s_cmp_pos_v, s_cmp_k_w1, s_cmp_k_b1, s_cmp_k_w2, s_cmp_v_w1, s_cmp_v_b1, s_cmp_v_w2, p_w_q, p_subkeys, p_u, p_v, ln_g, ln_b):
    batch, seq, d = x.shape
    assert d == D_MODEL and seq % 512 == 0 and seq // SEL_LEN <= LANES and seq // CMP_STRIDE == LANES
    h = x.reshape(batch * seq, d)

    qp, kp, vp = _mla_proj(h, a_w_in[0], a_q_norm[0], a_kv_norm[0], a_w_q_up[0], a_w_kv_up[0], seq)
    o = _mla_attn(qp, kp, vp, batch, seq)
    h = _proj_ln(o, a_w_o[0], h, ln_g[0, 0], ln_b[0, 0])
    h = _peer(h, p_w_q[0], p_subkeys[0], p_u[0], p_v[0], ln_g[0, 1], ln_b[0, 1])

    kvc, ks, vs, kw, vw, q, qr, gates = _nsa_proj(h, s_w_kv, b_w_in[0], seq)
    cmp = _nsa_compress(kvc, s_cmp_pos_k, s_cmp_pos_v, s_cmp_k_w1, s_cmp_k_b1, s_cmp_k_w2,
                        s_cmp_v_w1, s_cmp_v_b1, s_cmp_v_w2, batch, seq)
    o = _nsa_attn(q, qr, gates, ks, vs, kw, vw, cmp, batch, seq)
    h = _proj_ln(o, b_w_o[0], h, ln_g[1, 0], ln_b[1, 0])
    h = _peer(h, p_w_q[1], p_subkeys[1], p_u[1], p_v[1], ln_g[1, 1], ln_b[1, 1])
    return h.reshape(batch, seq, d)
```

```python
import functools
import math

import jax
import jax.numpy as jnp
from jax import lax
from jax.experimental import pallas as pl
from jax.experimental.pallas import tpu as pltpu

F32 = jnp.float32
BF16 = jnp.bfloat16

LANES = 128
SUBLANES = 8
V7X_VMEM_LIMIT_BYTES = 56 * 1024 * 1024

D_MODEL = 1024
DEPTH = 2
ALPHA = (2.0 * DEPTH) ** 0.25
LN_EPS = 1e-5
RMS_EPS = 1e-6
ROPE_THETA = 10000.0
NEG_INF = -1e30
FORCE_SCORE = 1e9
TINY = 1e-30

A_HEADS = 16
A_NOPE = 64
A_ROPE = 32
A_V = 64
A_Q_LORA = 512
A_KV_LORA = 256

B_HEADS = 16
B_GROUPS = 4
B_HPG = 4
B_DK = 64
B_DV = 64
CMP_LEN = 32
CMP_STRIDE = 16
CMP_HID = 256
SEL_LEN = 64
SEL_SHIFT = 6
SEL_N = 16
SEL_LOCAL = 2
WINDOW = 512

PEER_HEADS = 8
PEER_TOPK = 16
N_KEYS = 128
PEER_HALF = 128

_NT = (((1,), (1,)), ((), ()))


def _params(sem):
    return pltpu.CompilerParams(dimension_semantics=sem, vmem_limit_bytes=V7X_VMEM_LIMIT_BYTES)


def _layer_norm(y, g, b):
    mu = jnp.mean(y, axis=-1, keepdims=True)
    yc = y - mu
    var = jnp.mean(yc * yc, axis=-1, keepdims=True)
    return yc * lax.rsqrt(var + LN_EPS) * g + b


def _rms_norm(y, g):
    r = lax.rsqrt(jnp.mean(y * y, axis=-1, keepdims=True) + RMS_EPS)
    return y * r * g


def _rope_block(xb, c, s_lo, s_hi, half):
    up = pltpu.roll(xb, LANES - half, 1)
    dn = pltpu.roll(xb, half, 1)
    return xb * c + up * s_lo + dn * s_hi


def _rope_tables(seq, d_rot, lane_offsets, block_pass_lanes):
    half = d_rot // 2
    inv = ROPE_THETA ** (-jnp.arange(0, d_rot, 2, dtype=F32) / d_rot)
    ang = jnp.arange(seq, dtype=F32)[:, None] * inv[None, :]
    cos, sin = jnp.cos(ang), jnp.sin(ang)
    c = jnp.full((seq, LANES), 1.0 if block_pass_lanes else 0.0, F32)
    s_lo = jnp.zeros((seq, LANES), F32)
    s_hi = jnp.zeros((seq, LANES), F32)
    for off in lane_offsets:
        c = c.at[:, off:off + half].set(cos).at[:, off + half:off + d_rot].set(cos)
        s_lo = s_lo.at[:, off:off + half].set(-sin)
        s_hi = s_hi.at[:, off + half:off + d_rot].set(sin)
    return c, s_lo, s_hi


def _mla_proj_kernel(x_ref, win_ref, qn_ref, kvn_ref, wq_ref, wk_ref, wv_ref,
                     c_ref, slo_ref, shi_ref, q_out, k_out, v_out):
    x = x_ref[...].astype(BF16)
    proj = jnp.dot(x, win_ref[...], preferred_element_type=F32)
    c_q = _rms_norm(proj[:, :A_Q_LORA], qn_ref[...]).astype(BF16)
    c_kv = _rms_norm(proj[:, A_Q_LORA:A_Q_LORA + A_KV_LORA], kvn_ref[...]).astype(BF16)
    c, slo, shi = c_ref[...], slo_ref[...], shi_ref[...]
    kpe = _rope_block(proj[:, A_Q_LORA + A_KV_LORA:], c, slo, shi, A_ROPE // 2)
    q = jnp.dot(c_q, wq_ref[...], preferred_element_type=F32)
    kn = jnp.dot(c_kv, wk_ref[...], preferred_element_type=F32)
    vv = jnp.dot(c_kv, wv_ref[...], preferred_element_type=F32)
    scale = (A_NOPE + A_ROPE) ** -0.5
    lane = lax.broadcasted_iota(jnp.int32, (1, LANES), 1)
    for h in range(A_HEADS):
        sl = slice(h * LANES, (h + 1) * LANES)
        q_out[:, sl] = (_rope_block(q[:, sl], c, slo, shi, A_ROPE // 2) * scale).astype(BF16)
        k_out[:, sl] = (kn[:, sl] + kpe).astype(BF16)
        v_out[:, sl] = jnp.where(lane == A_V, 1.0, vv[:, sl]).astype(BF16)


def _mla_proj(x2, w_in, q_norm, kv_norm, w_q_up, w_kv_up, seq, tm=512):
    t = x2.shape[0]
    win_p = jnp.zeros((D_MODEL, A_Q_LORA + A_KV_LORA + LANES), F32)
    win_p = win_p.at[:, :A_Q_LORA + A_KV_LORA].set(w_in[:, :A_Q_LORA + A_KV_LORA])
    win_p = win_p.at[:, A_Q_LORA + A_KV_LORA + A_NOPE:A_Q_LORA + A_KV_LORA + A_NOPE + A_ROPE].set(
        w_in[:, A_Q_LORA + A_KV_LORA:])
    wq = jnp.pad(w_q_up.reshape(A_Q_LORA, A_HEADS, A_NOPE + A_ROPE),
                 ((0, 0), (0, 0), (0, LANES - A_NOPE - A_ROPE))).reshape(A_Q_LORA, A_HEADS * LANES)
    wkv = w_kv_up.reshape(A_KV_LORA, A_HEADS, A_NOPE + A_V)
    wk = jnp.pad(wkv[:, :, :A_NOPE], ((0, 0), (0, 0), (0, LANES - A_NOPE))).reshape(A_KV_LORA, A_HEADS * LANES)
    wv = jnp.pad(wkv[:, :, A_NOPE:], ((0, 0), (0, 0), (0, LANES - A_V))).reshape(A_KV_LORA, A_HEADS * LANES)
    c, slo, shi = _rope_tables(seq, A_ROPE, (A_NOPE,), True)
    nsb = seq // tm
    full = lambda shp: pl.BlockSpec(shp, lambda i: (0, 0))
    tab = pl.BlockSpec((tm, LANES), lambda i: (i % nsb, 0))
    return pl.pallas_call(
        _mla_proj_kernel,
        grid=(t // tm,),
        in_specs=[pl.BlockSpec((tm, D_MODEL), lambda i: (i, 0)),
                  full(win_p.shape), full((1, A_Q_LORA)), full((1, A_KV_LORA)),
                  full(wq.shape), full(wk.shape), full(wv.shape), tab, tab, tab],
        out_specs=[pl.BlockSpec((tm, A_HEADS * LANES), lambda i: (i, 0)),
                   pl.BlockSpec((tm, A_HEADS * LANES), lambda i: (i, 0)),
                   pl.BlockSpec((tm, A_HEADS * LANES), lambda i: (i, 0))],
        out_shape=[jax.ShapeDtypeStruct((t, A_HEADS * LANES), BF16)] * 3,
        compiler_params=_params(("parallel",)),
        name="mla_proj",
    )(x2, win_p.astype(BF16), q_norm.reshape(1, -1), kv_norm.reshape(1, -1),
      wq.astype(BF16), wk.astype(BF16), wv.astype(BF16), c, slo, shi)


def _lane_blocks(s):
    return [s[:, c * LANES:(c + 1) * LANES] for c in range(s.shape[1] // LANES)]


def _max_pass_step(s_slabs, mask, macc_ref):
    tq = s_slabs[0].shape[0]
    for i, s in enumerate(s_slabs):
        rs = slice(i * tq, (i + 1) * tq)
        if mask is not None:
            s = jnp.where(mask, s, NEG_INF)
        m = macc_ref[rs, :]
        for blk in _lane_blocks(s):
            m = jnp.maximum(m, blk)
        macc_ref[rs, :] = m


def _finish_max_pass(macc_ref):
    macc_ref[...] = jnp.broadcast_to(jnp.max(macc_ref[...], axis=1, keepdims=True), macc_ref.shape)


def _prob_slabs(s_slabs, mask, macc_ref):
    tq = s_slabs[0].shape[0]
    ps = []
    for i, s in enumerate(s_slabs):
        mb = macc_ref[i * tq:(i + 1) * tq, :]
        p = jnp.concatenate([jnp.exp(blk - mb) for blk in _lane_blocks(s)], axis=1)
        if mask is not None:
            p = jnp.where(mask, p, 0.0)
        ps.append(p.astype(BF16))
    return ps


def _run_tile_loops(tiles, step):
    for lo, hi, mask_fn in tiles:
        if hi is None:
            step(lo, mask_fn)
        else:
            lax.fori_loop(lo, hi, lambda j, c, mask_fn=mask_fn: (step(j, mask_fn), c)[1], 0)


def _mla_attn_kernel(q_ref, k_ref, v_ref, o_ref, macc_ref, acc_ref, *, tq):
    qi = pl.program_id(2)
    row = qi * tq + lax.broadcasted_iota(jnp.int32, (tq, 1), 0)

    def scores(j):
        ds = pl.ds(pl.multiple_of(j * tq, tq), tq)
        return ds, [lax.dot_general(q_ref[:, hh * LANES:(hh + 1) * LANES], k_ref[ds, hh * LANES:(hh + 1) * LANES],
                                    _NT, preferred_element_type=F32) for hh in range(2)]

    def causal(j):
        return j * tq + lax.broadcasted_iota(jnp.int32, (1, tq), 1) <= row

    tiles = [(0, qi, None), (qi, None, causal)]
    macc_ref[...] = jnp.full(macc_ref.shape, NEG_INF, F32)
    acc_ref[...] = jnp.zeros(acc_ref.shape, F32)

    def max_step(j, mask_fn):
        _, s = scores(j)
        _max_pass_step(s, None if mask_fn is None else mask_fn(j), macc_ref)

    def acc_step(j, mask_fn):
        ds, s = scores(j)
        ps = _prob_slabs(s, None if mask_fn is None else mask_fn(j), macc_ref)
        for hh in range(2):
            rs = slice(hh * tq, (hh + 1) * tq)
            acc_ref[rs, :] += jnp.dot(ps[hh], v_ref[ds, hh * LANES:(hh + 1) * LANES], preferred_element_type=F32)

    _run_tile_loops(tiles, max_step)
    _finish_max_pass(macc_ref)
    _run_tile_loops(tiles, acc_step)
    acc = acc_ref[...]
    out = acc / acc[:, A_V:A_V + 1]
    lane = lax.broadcasted_iota(jnp.int32, (1, LANES), 1)
    o_ref[...] = jnp.where(lane < A_V, out[:tq], pltpu.roll(out[tq:], A_V, 1)).astype(o_ref.dtype)


def _mla_attn(qp, kp, vp, batch, seq, tq=512):
    t = qp.shape[0]
    nq = seq // tq
    npair = A_HEADS // 2
    return pl.pallas_call(
        functools.partial(_mla_attn_kernel, tq=tq),
        grid=(batch, npair, nq),
        in_specs=[pl.BlockSpec((tq, 2 * LANES), lambda b, p, i: (b * nq + i, p)),
                  pl.BlockSpec((seq, 2 * LANES), lambda b, p, i: (b, p)),
                  pl.BlockSpec((seq, 2 * LANES), lambda b, p, i: (b, p))],
        out_specs=pl.BlockSpec((tq, LANES), lambda b, p, i: (b * nq + i, p)),
        out_shape=jax.ShapeDtypeStruct((t, A_HEADS * A_V), BF16),
        scratch_shapes=[pltpu.VMEM((2 * tq, LANES), F32), pltpu.VMEM((2 * tq, LANES), F32)],
        compiler_params=_params(("parallel", "parallel", "arbitrary")),
        name="mla_attn",
    )(qp, kp, vp)


def _proj_ln_kernel(o_ref, w_ref, res_ref, g_ref, b_ref, out_ref):
    mix = jnp.dot(o_ref[...], w_ref[...], preferred_element_type=F32)
    out_ref[...] = _layer_norm(ALPHA * res_ref[...] + mix, g_ref[...], b_ref[...])


def _proj_ln(o, w, res, g, b, tm=512):
    t = o.shape[0]
    row = lambda i: (i, 0)
    full = lambda shp: pl.BlockSpec(shp, lambda i: (0, 0))
    return pl.pallas_call(
        _proj_ln_kernel,
        grid=(t // tm,),
        in_specs=[pl.BlockSpec((tm, D_MODEL), row), full(w.shape), pl.BlockSpec((tm, D_MODEL), row),
                  full((1, D_MODEL)), full((1, D_MODEL))],
        out_specs=pl.BlockSpec((tm, D_MODEL), row),
        out_shape=jax.ShapeDtypeStruct((t, D_MODEL), F32),
        compiler_params=_params(("parallel",)),
        name="proj_ln",
    )(o, w.astype(BF16), res, g.reshape(1, -1), b.reshape(1, -1))


_PEER_NEXT = PEER_TOPK + 1
_PEER_CAND = [(a, b) for a in range(_PEER_NEXT) for b in range(_PEER_NEXT) if (a + 1) * (b + 1) <= _PEER_NEXT]
_PEER_CAND_ROWS = ((len(_PEER_CAND) + 7) // 8) * 8
_BIG_NEG = -3.0e38


def _extract_max_rows(cur, n, dst_ref):
    for k in range(n):
        m = jnp.max(cur, axis=0, keepdims=True)
        dst_ref[k:k + 1, :] = m
        cur = jnp.where(cur == m, _BIG_NEG, cur)


def _peer_route(h_ref, wqT_ref, sk_ref, hT_sc, qT_sc, s2_sc, e2_sc, thr_sc, e1_sc, top_sc, cand_sc, best_sc):
    hT = h_ref[...].T.astype(BF16)
    hT_sc[...] = hT
    qT_sc[...] = jnp.dot(wqT_ref[...], hT, preferred_element_type=F32).astype(BF16)
    cand_sc[...] = jnp.full(cand_sc.shape, _BIG_NEG, F32)

    def head(hh, carry):
        s = []
        for p in range(2):
            qb = qT_sc[pl.ds(pl.multiple_of((hh * 2 + p) * PEER_HALF, PEER_HALF), PEER_HALF), :]
            sp = jnp.dot(sk_ref[hh * 2 + p], qb, preferred_element_type=F32)
            _extract_max_rows(sp, _PEER_NEXT, top_sc.at[p])
            s.append(sp)
        for r, (a, b) in enumerate(_PEER_CAND):
            cand_sc[r:r + 1, :] = top_sc[0, a:a + 1, :] + top_sc[1, b:b + 1, :]
        _extract_max_rows(cand_sc[...], _PEER_NEXT, best_sc)
        m = best_sc[0:1, :]
        z = jnp.sum(jnp.exp(best_sc[0:PEER_TOPK, :] - m), axis=0, keepdims=True)
        tau = 0.5 * (best_sc[PEER_TOPK - 1:PEER_TOPK, :] + best_sc[PEER_TOPK:PEER_TOPK + 1, :])
        thr_sc[hh] = tau - s[0]
        e1_sc[hh] = jnp.exp(s[0] - top_sc[0, 0:1, :])
        e2_sc[hh] = jnp.exp(s[1] - top_sc[1, 0:1, :]) / z
        s2_sc[hh] = s[1]
        return carry

    lax.fori_loop(0, PEER_HEADS, head, 0)


def _peer_kernel(h_ref, wqT_ref, sk_ref, u_ref, vT_ref, g_ref, b_ref, out_ref,
                 hT_sc, qT_sc, s2_sc, e2_sc, thr_sc, e1_sc, top_sc, cand_sc, best_sc, gate_sc, acc_sc,
                 *, tt, eb):
    e = pl.program_id(1)

    @pl.when(e == 0)
    def _():
        _peer_route(h_ref, wqT_ref, sk_ref, hT_sc, qT_sc, s2_sc, e2_sc, thr_sc, e1_sc, top_sc, cand_sc, best_sc)
        acc_sc[...] = jnp.zeros(acc_sc.shape, F32)

    hT = hT_sc[...]
    nblk = eb // N_KEYS
    rows = pl.ds(pl.multiple_of(e * nblk, nblk), nblk)
    for blk in range(nblk):
        hid = jnp.dot(u_ref[blk * N_KEYS:(blk + 1) * N_KEYS, :], hT, preferred_element_type=F32)
        for lc in range(tt // LANES):
            ls = slice(lc * LANES, (lc + 1) * LANES)
            w = jnp.zeros((N_KEYS, LANES), F32)
            for hh in range(PEER_HEADS):
                thr = thr_sc[hh, rows, ls][blk:blk + 1, :]
                e1 = e1_sc[hh, rows, ls][blk:blk + 1, :]
                w = w + jnp.where(s2_sc[hh, :, ls] >= thr, e2_sc[hh, :, ls] * e1, 0.0)
            hb = hid[:, ls]
            act = 0.5 * hb * (1.0 + lax.erf(hb * (2.0 ** -0.5)))
            gate_sc[blk * N_KEYS:(blk + 1) * N_KEYS, ls] = (act * w).astype(BF16)
    acc_sc[...] += jnp.dot(vT_ref[...], gate_sc[...], preferred_element_type=F32)

    @pl.when(e == pl.num_programs(1) - 1)
    def _():
        y = ALPHA * h_ref[...] + acc_sc[...].T
        out_ref[...] = _layer_norm(y, g_ref[...], b_ref[...])


def _peer(h, w_q, subkeys, u, v, g, b, tt=512, eb=SUBLANES * N_KEYS):
    t = h.shape[0]
    n_exp = u.shape[0]
    assert eb == SUBLANES * N_KEYS and n_exp == N_KEYS * N_KEYS
    wqT = w_q.T.astype(BF16)
    sk = subkeys.reshape(PEER_HEADS * 2, N_KEYS, PEER_HALF).astype(BF16)
    vT = v.T.astype(BF16)
    return pl.pallas_call(
        functools.partial(_peer_kernel, tt=tt, eb=eb),
        grid=(t // tt, n_exp // eb),
        in_specs=[pl.BlockSpec((tt, D_MODEL), lambda i, e: (i, 0)),
                  pl.BlockSpec(wqT.shape, lambda i, e: (0, 0)),
                  pl.BlockSpec(sk.shape, lambda i, e: (0, 0, 0)),
                  pl.BlockSpec((eb, D_MODEL), lambda i, e: (e, 0)),
                  pl.BlockSpec((D_MODEL, eb), lambda i, e: (0, e)),
                  pl.BlockSpec((1, D_MODEL), lambda i, e: (0, 0)),
                  pl.BlockSpec((1, D_MODEL), lambda i, e: (0, 0))],
        out_specs=pl.BlockSpec((tt, D_MODEL), lambda i, e: (i, 0)),
        out_shape=jax.ShapeDtypeStruct((t, D_MODEL), F32),
        scratch_shapes=[pltpu.VMEM((D_MODEL, tt), BF16),
                        pltpu.VMEM((PEER_HEADS * 2 * PEER_HALF, tt), BF16),
                        pltpu.VMEM((PEER_HEADS, N_KEYS, tt), F32),
                        pltpu.VMEM((PEER_HEADS, N_KEYS, tt), F32),
                        pltpu.VMEM((PEER_HEADS, N_KEYS, tt), F32),
                        pltpu.VMEM((PEER_HEADS, N_KEYS, tt), F32),
                        pltpu.VMEM((2, 24, tt), F32),
                        pltpu.VMEM((_PEER_CAND_ROWS, tt), F32),
                        pltpu.VMEM((24, tt), F32),
                        pltpu.VMEM((eb, tt), BF16),
                        pltpu.VMEM((D_MODEL, tt), F32)],
        compiler_params=_params(("parallel", "arbitrary")),
        name="peer",
    )(h, wqT, sk, u.astype(BF16), vT, g.reshape(1, -1), b.reshape(1, -1))


_KV_COLS = 3 * B_GROUPS * (B_DK + B_DV)
_Q_COLS = B_HEADS * B_DK
_GATE_COLS = B_GROUPS * LANES


def _nsa_proj_kernel(h_ref, w_ref, ck_ref, sklo_ref, skhi_ref, cq_ref, sqlo_ref, sqhi_ref,
                     kvc_out, ks_out, vs_out, kw_out, vw_out, q_out, qr_out, gate_out):
    h = h_ref[...].astype(BF16)
    proj = jnp.dot(h, w_ref[...], preferred_element_type=F32)
    nb = B_GROUPS * LANES
    kvc_out[...] = proj[:, :nb]
    ck, sklo, skhi = ck_ref[...], sklo_ref[...], skhi_ref[...]
    lane = lax.broadcasted_iota(jnp.int32, (1, LANES), 1)
    for br, (k_out, v_out) in ((1, (ks_out, vs_out)), (2, (kw_out, vw_out))):
        for g in range(B_GROUPS):
            sl = slice(br * nb + g * LANES, br * nb + (g + 1) * LANES)
            kv = _rope_block(proj[:, sl], ck, sklo, skhi, B_DK // 2)
            sw = pltpu.roll(kv, B_DK, 1)
            gs = slice(g * LANES, (g + 1) * LANES)
            k_out[:, gs] = jnp.where(lane < B_DK, kv, sw).astype(BF16)
            ones_lane = jnp.where(lane == B_DV, 1.0, 0.0)
            v_out[:, gs] = jnp.where(lane < B_DK, sw, ones_lane).astype(BF16)
    cq, sqlo, sqhi = cq_ref[...], sqlo_ref[...], sqhi_ref[...]
    scale = B_DK ** -0.5
    for j in range(_Q_COLS // LANES):
        sl = slice(_KV_COLS + j * LANES, _KV_COLS + (j + 1) * LANES)
        js = slice(j * LANES, (j + 1) * LANES)
        qb = proj[:, sl]
        q_out[:, js] = (qb * scale).astype(BF16)
        qr_out[:, js] = (_rope_block(qb, cq, sqlo, sqhi, B_DK // 2) * scale).astype(BF16)
    gz = proj[:, _KV_COLS + _Q_COLS:]
    gate_out[...] = 1.0 / (1.0 + jnp.exp(-gz))


def _nsa_proj(h, w_kv, w_in, seq, tm=512):
    t = h.shape[0]
    wg = w_in[:, _Q_COLS:].reshape(D_MODEL, 3, B_GROUPS, B_HPG).transpose(0, 2, 1, 3)
    wg = jnp.pad(wg.reshape(D_MODEL, B_GROUPS, 3 * B_HPG), ((0, 0), (0, 0), (0, LANES - 3 * B_HPG)))
    w = jnp.concatenate([w_kv, w_in[:, :_Q_COLS], wg.reshape(D_MODEL, _GATE_COLS)], axis=1).astype(BF16)
    ck, sklo, skhi = _rope_tables(seq, B_DK, (0,), True)
    cq, sqlo, sqhi = _rope_tables(seq, B_DK, (0, B_DK), True)
    nsb = seq // tm
    row = lambda i: (i, 0)
    tab = pl.BlockSpec((tm, LANES), lambda i: (i % nsb, 0))
    nb = B_GROUPS * LANES
    shapes = [((t, nb), F32)] + [((t, nb), BF16)] * 4 + [((t, _Q_COLS), BF16)] * 2 + [((t, _GATE_COLS), F32)]
    return pl.pallas_call(
        _nsa_proj_kernel,
        grid=(t // tm,),
        in_specs=[pl.BlockSpec((tm, D_MODEL), row), pl.BlockSpec(w.shape, lambda i: (0, 0))] + [tab] * 6,
        out_specs=[pl.BlockSpec((tm, s[1]), row) for s, _ in shapes],
        out_shape=[jax.ShapeDtypeStruct(s, d) for s, d in shapes],
        compiler_params=_params(("parallel",)),
        name="nsa_proj",
    )(h, w, ck, sklo, skhi, cq, sqlo, sqhi)


def _nsa_compress_kernel(x_ref, pos_ref, w1_ref, b1_ref, w2_ref, out_ref):
    x = x_ref[0, 0]
    top = jnp.dot((x + pos_ref[0, 0:1, :]).astype(BF16), w1_ref[0, 0], preferred_element_type=F32)
    bot = jnp.dot((x + pos_ref[0, 1:2, :]).astype(BF16), w1_ref[0, 1], preferred_element_type=F32)
    nrow = x.shape[0]
    hid = top + pltpu.roll(bot, nrow - 1, 0) + b1_ref[0]
    act = 0.5 * hid * (1.0 + lax.erf(hid * (2.0 ** -0.5)))
    out_ref[0, 0] = jnp.dot(act.astype(BF16), w2_ref[0], preferred_element_type=F32).astype(BF16)


def _nsa_compress(kvc, pos_k, pos_v, k_w1, k_b1, k_w2, v_w1, v_b1, v_w2, batch, seq):
    nsb = seq // CMP_STRIDE
    feat = CMP_STRIDE * B_DK
    x = kvc.reshape(batch, nsb, CMP_STRIDE, B_GROUPS, 2, B_DK).transpose(0, 3, 4, 1, 2, 5)
    x = x.reshape(batch, B_GROUPS * 2, nsb, feat)
    pos = jnp.stack([pos_k.reshape(2, feat), pos_v.reshape(2, feat)])
    w1 = jnp.stack([k_w1.reshape(2, feat, CMP_HID), v_w1.reshape(2, feat, CMP_HID)]).astype(BF16)
    b1 = jnp.stack([k_b1.reshape(1, CMP_HID), v_b1.reshape(1, CMP_HID)])
    w2 = jnp.stack([jnp.concatenate([k_w2, k_w2], axis=1), jnp.concatenate([v_w2, v_w2], axis=1)]).astype(BF16)
    return pl.pallas_call(
        _nsa_compress_kernel,
        grid=(batch, B_GROUPS * 2),
        in_specs=[pl.BlockSpec((1, 1, nsb, feat), lambda b, c: (b, c, 0, 0)),
                  pl.BlockSpec((1, 2, feat), lambda b, c: (c % 2, 0, 0)),
                  pl.BlockSpec((1, 2, feat, CMP_HID), lambda b, c: (c % 2, 0, 0, 0)),
                  pl.BlockSpec((1, 1, CMP_HID), lambda b, c: (c % 2, 0, 0)),
                  pl.BlockSpec((1, CMP_HID, LANES), lambda b, c: (c % 2, 0, 0))],
        out_specs=pl.BlockSpec((1, 1, nsb, LANES), lambda b, c: (b, c, 0, 0)),
        out_shape=jax.ShapeDtypeStruct((batch, B_GROUPS * 2, nsb, LANES), BF16),
        compiler_params=_params(("parallel", "arbitrary")),
        name="nsa_compress",
    )(x, pos, w1, b1, w2)


def _split_bf16(x):
    hi = x.astype(BF16)
    mid = (x - hi.astype(F32)).astype(BF16)
    lo = (x - hi.astype(F32) - mid.astype(F32)).astype(BF16)
    return hi, mid, lo


def _nsa_attn_kernel(q_ref, qr_ref, gate_ref, ks_ref, vs_ref, kw_ref, vw_ref, cmp_ref, o_ref,
                     macc_ref, acc_ref, *, tq, seq):
    qi = pl.program_id(2)
    t0 = qi * tq
    row = t0 + lax.broadcasted_iota(jnp.int32, (tq, 1), 0)
    lane = lax.broadcasted_iota(jnp.int32, (1, LANES), 1)
    n_cmp = seq // CMP_STRIDE - CMP_LEN // CMP_STRIDE + 1
    n_blk = seq // SEL_LEN
    kc = cmp_ref[0, 0]
    vc = cmp_ref[0, 1]
    gates = gate_ref[...]

    def stack_q(ref):
        parts = []
        for head in range(B_HPG):
            pair, hh = divmod(head, 2)
            qp = ref[:, pair * LANES:(pair + 1) * LANES]
            keep = (lane < B_DK) if hh == 0 else (lane >= B_DK)
            parts.append(jnp.where(keep, qp, jnp.zeros_like(qp)))
        return jnp.concatenate(parts, axis=0)

    def stack_gate(branch):
        cols = [gates[:, branch * B_HPG + hd:branch * B_HPG + hd + 1] for hd in range(B_HPG)]
        return jnp.concatenate(cols, axis=0)

    q_all = stack_q(q_ref)
    qr_all = stack_q(qr_ref)

    cmask = (lane * CMP_STRIDE + (CMP_LEN - 1) <= row) & (lane < n_cmp)
    s_c = lax.dot_general(q_all, kc, _NT, preferred_element_type=F32)
    ps = []
    psum = jnp.zeros((tq, LANES), F32)
    for hd in range(B_HPG):
        s = jnp.where(cmask, s_c[hd * tq:(hd + 1) * tq], NEG_INF)
        ex = jnp.where(cmask, jnp.exp(s - jnp.max(s, axis=1, keepdims=True)), 0.0)
        p = ex / jnp.maximum(jnp.sum(ex, axis=1, keepdims=True), TINY)
        psum = psum + p
        ps.append(p.astype(BF16))
    out = stack_gate(0) * jnp.dot(jnp.concatenate(ps, axis=0), vc, preferred_element_type=F32)
    crow = lax.broadcasted_iota(jnp.int32, (LANES, LANES), 0) * CMP_STRIDE
    ncol = lax.broadcasted_iota(jnp.int32, (LANES, LANES), 1) * SEL_LEN
    overlap = jnp.where((crow < ncol + SEL_LEN) & (crow + (CMP_LEN - 1) >= ncol), 1.0, 0.0).astype(BF16)
    imp = sum(jnp.dot(part, overlap, preferred_element_type=F32) for part in _split_bf16(psum))
    cur = jnp.right_shift(row, SEL_SHIFT)
    forced = (lane == 0) | ((lane <= cur) & (lane > cur - SEL_LOCAL))
    imp = jnp.where(forced, FORCE_SCORE, imp)
    imp = jnp.where((lane <= cur) & (lane < n_blk), imp, NEG_INF)
    imp_t = imp.T
    blk = lax.broadcasted_iota(jnp.int32, (LANES, 1), 0)
    rank = jnp.zeros((LANES, tq), F32)
    for mblk in range(n_blk):
        other = imp_t[mblk:mblk + 1, :]
        ahead = (other > imp_t) | ((other == imp_t) & (blk > mblk))
        rank = rank + jnp.where(ahead, 1.0, 0.0)
    sel_t = jnp.where((rank < min(SEL_N, n_blk)) & (blk < n_blk), 1.0, 0.0)
    sel = sel_t.T.astype(BF16)

    def run_branch(k_ref, v_ref, tiles, branch):
        def scores(j):
            ds = pl.ds(pl.multiple_of(j * tq, tq), tq)
            s = lax.dot_general(qr_all, k_ref[ds, :], _NT, preferred_element_type=F32)
            return ds, [s[hd * tq:(hd + 1) * tq] for hd in range(B_HPG)]

        def max_step(j, mask_fn):
            _, s = scores(j)
            _max_pass_step(s, None if mask_fn is None else mask_fn(j), macc_ref)

        def acc_step(j, mask_fn):
            ds, s = scores(j)
            ps = _prob_slabs(s, None if mask_fn is None else mask_fn(j), macc_ref)
            acc_ref[...] += jnp.dot(jnp.concatenate(ps, axis=0), v_ref[ds, :], preferred_element_type=F32)

        macc_ref[...] = jnp.full(macc_ref.shape, NEG_INF, F32)
        acc_ref[...] = jnp.zeros(acc_ref.shape, F32)
        _run_tile_loops(tiles, max_step)
        _finish_max_pass(macc_ref)
        _run_tile_loops(tiles, acc_step)
        acc = acc_ref[...]
        return (stack_gate(branch) / acc[:, B_DV:B_DV + 1]) * acc

    def key_pos(j):
        return j * tq + lax.broadcasted_iota(jnp.int32, (1, tq), 1)

    def chosen(j):
        blk_of_key = jnp.right_shift(j * tq + lax.broadcasted_iota(jnp.int32, (LANES, tq), 1), SEL_SHIFT)
        expand = jnp.where(blk_of_key == lax.broadcasted_iota(jnp.int32, (LANES, tq), 0), 1.0, 0.0).astype(BF16)
        return jnp.dot(sel, expand, preferred_element_type=F32) > 0.5

    n_free = jnp.where(t0 + tq <= SEL_N * SEL_LEN, qi, 0)
    out = out + run_branch(ks_ref, vs_ref,
                           [(0, n_free, None), (n_free, qi, chosen),
                            (qi, None, lambda j: chosen(j) & (key_pos(j) <= row))], 1)
    n_win = WINDOW // tq
    out = out + run_branch(kw_ref, vw_ref,
                           [(jnp.maximum(qi - n_win, 0), jnp.maximum(qi - n_win + 1, 0),
                             lambda j: key_pos(j) > row - WINDOW),
                            (jnp.maximum(qi - n_win + 1, 0), qi, None),
                            (qi, None, lambda j: key_pos(j) <= row)], 2)
    for pair in range(B_HPG // 2):
        lo = out[(2 * pair) * tq:(2 * pair + 1) * tq]
        hi = pltpu.roll(out[(2 * pair + 1) * tq:(2 * pair + 2) * tq], B_DV, 1)
        o_ref[:, pair * LANES:(pair + 1) * LANES] = jnp.where(lane < B_DV, lo, hi).astype(o_ref.dtype)


def _nsa_attn(q, qr, gates, ks, vs, kw, vw, cmp, batch, seq, tq=256):
    t = q.shape[0]
    assert WINDOW % tq == 0 and (SEL_N * SEL_LEN) % tq == 0 and tq % SEL_LEN == 0
    nq = seq // tq
    qcols = B_HPG * B_DK
    qspec = pl.BlockSpec((tq, qcols), lambda b, g, i: (b * nq + i, g))
    kvspec = pl.BlockSpec((seq, LANES), lambda b, g, i: (b, g))
    return pl.pallas_call(
        functools.partial(_nsa_attn_kernel, tq=tq, seq=seq),
        grid=(batch, B_GROUPS, nq),
        in_specs=[qspec, qspec, pl.BlockSpec((tq, LANES), lambda b, g, i: (b * nq + i, g)),
                  kvspec, kvspec, kvspec, kvspec,
                  pl.BlockSpec((1, 2, seq // CMP_STRIDE, LANES), lambda b, g, i: (b, g, 0, 0))],
        out_specs=qspec,
        out_shape=jax.ShapeDtypeStruct((t, B_HEADS * B_DV), BF16),
        scratch_shapes=[pltpu.VMEM((B_HPG * tq, LANES), F32), pltpu.VMEM((B_HPG * tq, LANES), F32)],
        compiler_params=_params(("parallel", "parallel", "arbitrary")),
        name="nsa_attn",
    )(q, qr, gates, ks, vs, kw, vw, cmp)


def kernel(x, a_w_in, a_q_norm, a_kv_norm, a_w_q_up, a_w_kv_up, a_w_o, b_w_in, b_w_o, s_w_kv, s_cmp_pos_k, s_cmp_pos_v, s_cmp_k_w1, s_cmp_k_b1, s_cmp_k_w2, s_cmp_v_w1, s_cmp_v_b1, s_cmp_v_w2, p_w_q, p_subkeys, p_u, p_v, ln_g, ln_b):
    batch, seq, d = x.shape
    assert d == D_MODEL and seq % 512 == 0 and seq // SEL_LEN <= LANES and seq // CMP_STRIDE == LANES
    h = x.reshape(batch * seq, d)

    qp, kp, vp = _mla_proj(h, a_w_in[0], a_q_norm[0], a_kv_norm[0], a_w_q_up[0], a_w_kv_up[0], seq)
    o = _mla_attn(qp, kp, vp, batch, seq)
    h = _proj_ln(o, a_w_o[0], h, ln_g[0, 0], ln_b[0, 0])
    h = _peer(h, p_w_q[0], p_subkeys[0], p_u[0], p_v[0], ln_g[0, 1], ln_b[0, 1])

    kvc, ks, vs, kw, vw, q, qr, gates = _nsa_proj(h, s_w_kv, b_w_in[0], seq)
    cmp = _nsa_compress(kvc, s_cmp_pos_k, s_cmp_pos_v, s_cmp_k_w1, s_cmp_k_b1, s_cmp_k_w2,
                        s_cmp_v_w1, s_cmp_v_b1, s_cmp_v_w2, batch, seq)
    o = _nsa_attn(q, qr, gates, ks, vs, kw, vw, cmp, batch, seq)
    h = _proj_ln(o, b_w_o[0], h, ln_g[1, 0], ln_b[1, 0])
    h = _peer(h, p_w_q[1], p_subkeys[1], p_u[1], p_v[1], ln_g[1, 1], ln_b[1, 1])
    return h.reshape(batch, seq, d)
```
